```python
import math
import jax, jax.numpy as jnp
from jax import lax
import numpy as np

D_MODEL = 2048
BATCH = 2
SEQ = 4096
DEPTH = 1

ATT_HEADS = 8
ATT_QK_DIM = 64
ATT_V_DIM = 2 * ATT_QK_DIM
ATT_WIDTH = ATT_HEADS * ATT_V_DIM
ROPE_DIM = ATT_QK_DIM // 4
ROPE_THETA = 500000.0
Q_BLOCK = 128
RWKV_HEAD = 64
RWKV_WIDTH = D_MODEL // 2
RWKV_HEADS = RWKV_WIDTH // RWKV_HEAD
DECAY_RANK = 64
ICLR_RANK = 64
GATE_RANK = 128
QK_COLS = ATT_HEADS * 2 * ATT_QK_DIM
ATT_COLS = 2 * QK_COLS + ATT_WIDTH
RWKV_COLS = 3 * RWKV_WIDTH + DECAY_RANK + ICLR_RANK + GATE_RANK
GATE_COLS = 2 * D_MODEL
IN_COLS = ATT_COLS + RWKV_COLS + GATE_COLS
N_GROUPS = 8
EXPERTS_PER_GROUP = 8
N_EXPERTS = N_GROUPS * EXPERTS_PER_GROUP
EXPERT_FF = D_MODEL // 2
TOP_K = 2
EXPERT_BLOCK = 128
NORM_EPS = 1e-6
SUBLN_EPS = 1e-5
LNX_EPS = 64e-5

kernel_name = "hybrid_diffattn_rwkv7_hmoe"


def rmsnorm(x, g, eps=NORM_EPS):
    xf = x.astype(jnp.float32)
    y = xf * lax.rsqrt(jnp.mean(xf * xf, axis=-1, keepdims=True) + eps)
    return (y * g).astype(x.dtype)


def rope_partial(t, positions):
    half = ROPE_DIM // 2
    inv = ROPE_THETA ** (-jnp.arange(half, dtype=jnp.float32) / half)
    ang = positions.astype(jnp.float32)[..., None] * inv
    cos = jnp.cos(ang)[:, :, None, None, :]
    sin = jnp.sin(ang)[:, :, None, None, :]
    t1 = t[..., :half].astype(jnp.float32)
    t2 = t[..., half:ROPE_DIM].astype(jnp.float32)
    rot = jnp.concatenate([t1 * cos - t2 * sin, t1 * sin + t2 * cos], axis=-1).astype(t.dtype)
    return jnp.concatenate([rot, t[..., ROPE_DIM:]], axis=-1)


def diff_attention(q, k, v, lam, lambda_init, subln_g):
    B, S, H = v.shape[:3]
    nqb = S // Q_BLOCK
    qb = (q * (ATT_QK_DIM ** -0.5)).reshape(B, nqb, Q_BLOCK, H, 2, ATT_QK_DIM).transpose(1, 0, 3, 4, 2, 5)
    kt = k.transpose(0, 2, 3, 1, 4)
    vt = v.transpose(0, 2, 1, 3)
    kpos = jnp.arange(S)

    def one_block(args):
        qblk, i = args
        s = jnp.einsum('bhcqd,bhckd->bhcqk', qblk, kt, preferred_element_type=jnp.float32)
        qpos = i * Q_BLOCK + jnp.arange(Q_BLOCK)
        s = jnp.where(kpos[None, :] <= qpos[:, None], s, -jnp.inf)
        p = jax.nn.softmax(s, axis=-1)
        a = p[:, :, 0] - lam * p[:, :, 1]
        return jnp.einsum('bhqk,bhkd->bhqd', a.astype(vt.dtype), vt)

    o = lax.map(one_block, (qb, jnp.arange(nqb)))
    o = o.transpose(1, 0, 3, 2, 4).reshape(B, S, H, ATT_V_DIM)
    o = rmsnorm(o, subln_g, SUBLN_EPS) * (1.0 - lambda_init)
    return o.reshape(B, S, ATT_WIDTH)


def rwkv7_timemix(p, mu, w0, w2, a0, a2, g2, k_k, k_a, r_k, lnx_g, lnx_b):
    B, S, _ = p.shape
    f32 = jnp.float32
    prev = jnp.pad(p[:, :-1], ((0, 0), (1, 0), (0, 0)))
    p = p + (prev - p) * mu
    splits = [RWKV_WIDTH, 2 * RWKV_WIDTH, 3 * RWKV_WIDTH, 3 * RWKV_WIDTH + DECAY_RANK,
              3 * RWKV_WIDTH + DECAY_RANK + ICLR_RANK]
    r, k, v, wl, al, gl = jnp.split(p, splits, axis=-1)
    w_log = -jax.nn.softplus(-(w0 + jnp.tanh(wl) @ w2)) - 0.5
    decay = jnp.exp(-jnp.exp(w_log.astype(f32)))
    a = jax.nn.sigmoid(a0 + al @ a2)
    g = jax.nn.sigmoid(gl) @ g2
    heads = lambda t: t.reshape(B, S, RWKV_HEADS, RWKV_HEAD)
    kk = heads(k * k_k).astype(f32)
    kk = kk * lax.rsqrt(jnp.maximum(jnp.sum(kk * kk, axis=-1, keepdims=True), 1e-24))
    k = k * (1.0 + (a - 1.0) * k_a)
    rh, dh, kh, vh, ah = (heads(t).astype(f32) for t in (r, decay, k, v, a))
    xs = tuple(t.transpose(1, 0, 2, 3) for t in (rh, dh, kh, vh, -kk, kk * ah))

    def step(state, inp):
        r_t, w_t, k_t, v_t, a_t, b_t = inp
        sa = jnp.einsum('bhvk,bhk->bhv', state, a_t)
        state = state * w_t[:, :, None, :] + sa[..., None] * b_t[:, :, None, :] + v_t[..., None] * k_t[:, :, None, :]
        return state, jnp.einsum('bhvk,bhk->bhv', state, r_t)

    s0 = jnp.zeros((B, RWKV_HEADS, RWKV_HEAD, RWKV_HEAD), f32)
    _, ys = lax.scan(step, s0, xs)
    y = ys.transpose(1, 0, 2, 3)
    mean = jnp.mean(y, axis=-1, keepdims=True)
    var = jnp.mean(jnp.square(y - mean), axis=-1, keepdims=True)
    y = ((y - mean) * lax.rsqrt(var + LNX_EPS)).reshape(B, S, RWKV_WIDTH) * lnx_g + lnx_b
    bonus = jnp.sum(rh * kh * r_k, axis=-1, keepdims=True) * vh
    y = (y + bonus.reshape(B, S, RWKV_WIDTH)) * g
    return y.astype(p.dtype)


def hier_moe(h, w_rg, b_rg, w_re, b_re, w_gate, w_up, w_down):
    B, S, D = h.shape
    N = B * S
    f32 = jnp.float32
    hf = h.reshape(N, D)
    gprob = jax.nn.softmax((hf @ w_rg).astype(f32) + b_rg, axis=-1)
    gp, gidx = lax.top_k(gprob, 1)
    elog = ((hf @ w_re).astype(f32) + b_re).reshape(N, N_GROUPS, EXPERTS_PER_GROUP)
    elog = jnp.take_along_axis(elog, gidx[:, :, None], axis=1)[:, 0]
    ep, eidx = lax.top_k(jax.nn.softmax(elog, axis=-1), TOP_K)
    ew = (gp * ep / jnp.sum(ep, axis=-1, keepdims=True)).reshape(-1)
    expert_id = (gidx * EXPERTS_PER_GROUP + eidx).reshape(-1).astype(jnp.int32)
    flat_tok = jnp.repeat(jnp.arange(N, dtype=jnp.int32), TOP_K)
    A = N * TOP_K
    nblk = -(-A // EXPERT_BLOCK) + N_EXPERTS
    order = jnp.argsort(expert_id)
    se, stok, sw = expert_id[order], flat_tok[order], ew[order]
    counts = jnp.bincount(expert_id, length=N_EXPERTS).astype(jnp.int32)
    padded = (counts + EXPERT_BLOCK - 1) // EXPERT_BLOCK * EXPERT_BLOCK
    pad_end = jnp.cumsum(padded)
    pad_start = pad_end - padded
    start = jnp.cumsum(counts) - counts
    dest = pad_start[se] + jnp.arange(A, dtype=jnp.int32) - start[se]
    buf_tok = jnp.full((nblk * EXPERT_BLOCK,), N, jnp.int32).at[dest].set(stok)
    buf_w = jnp.zeros((nblk * EXPERT_BLOCK,), f32).at[dest].set(sw)
    blk_expert = jnp.minimum(jnp.searchsorted(pad_end, jnp.arange(nblk, dtype=jnp.int32) * EXPERT_BLOCK, side='right'), N_EXPERTS - 1)
    h_pad = jnp.concatenate([hf, jnp.zeros((1, D), hf.dtype)], axis=0)

    def run_block(args):
        tok, e = args
        xb = h_pad[tok]
        u = jax.nn.silu(xb @ w_gate[e]) * (xb @ w_up[e])
        return u @ w_down[e]

    yb = lax.map(run_block, (buf_tok.reshape(nblk, EXPERT_BLOCK), blk_expert))
    yb = yb.reshape(-1, D).astype(f32) * buf_w[:, None]
    out = jnp.zeros((N + 1, D), f32).at[buf_tok].add(yb)[:N]
    return out.reshape(B, S, D).astype(h.dtype)


def setup_inputs(seed: int = 0) -> dict:
    key = jax.random.key(seed)
    ks = iter(jax.random.split(key, 48))
    L, D = DEPTH, D_MODEL
    nrm = lambda shape, scale: jax.random.normal(next(ks), shape, jnp.float32) * scale
    x = nrm((BATCH, SEQ, D), 1.0)
    c = nrm((BATCH, D), 1.0)
    positions = jax.random.randint(next(ks), (BATCH, 1), 0, 4096, dtype=jnp.int32) + jnp.arange(SEQ, dtype=jnp.int32)[None, :]
    return {
        'x': x, 'c': c, 'positions': positions,
        'ada_w': nrm((L, D, 6 * D), 0.5 * D ** -0.5),
        'ada_b': nrm((L, 6 * D), 0.02),
        'norm_mix_g': 1.0 + nrm((L, D), 0.02),
        'w_in': nrm((L, D, IN_COLS), D ** -0.5),
        'tshift_mu': jax.random.uniform(next(ks), (L, RWKV_COLS), jnp.float32, 0.0, 1.0),
        'lambda_q1': nrm((L, ATT_QK_DIM), 0.1),
        'lambda_k1': nrm((L, ATT_QK_DIM), 0.1),
        'lambda_q2': nrm((L, ATT_QK_DIM), 0.1),
        'lambda_k2': nrm((L, ATT_QK_DIM), 0.1),
        'subln_g': 1.0 + nrm((L, ATT_V_DIM), 0.02),
        'w0': jax.random.uniform(next(ks), (L, RWKV_WIDTH), jnp.float32, -4.0, 0.0),
        'w2': nrm((L, DECAY_RANK, RWKV_WIDTH), 0.1),
        'a0': nrm((L, RWKV_WIDTH), 0.1),
        'a2': nrm((L, ICLR_RANK, RWKV_WIDTH), 0.1),
        'g2': nrm((L, GATE_RANK, RWKV_WIDTH), GATE_RANK ** -0.5),
        'k_k': 0.85 + nrm((L, RWKV_WIDTH), 0.02),
        'k_a': 1.0 + nrm((L, RWKV_WIDTH), 0.02),
        'r_k': nrm((L, RWKV_HEADS, RWKV_HEAD), 0.1),
        'lnx_g': 1.0 + nrm((L, RWKV_WIDTH), 0.02),
        'lnx_b': nrm((L, RWKV_WIDTH), 0.02),
        'w_up_attn': nrm((L, ATT_WIDTH, D), ATT_WIDTH ** -0.5),
        'w_up_rwkv': nrm((L, RWKV_WIDTH, D), RWKV_WIDTH ** -0.5),
        'w_out': nrm((L, D, D), D ** -0.5),
        'norm_ffn_g': 1.0 + nrm((L, D), 0.02),
        'w_route_group': nrm((L, D, N_GROUPS), D ** -0.5),
        'b_route_group': nrm((L, N_GROUPS), 0.01),
        'w_route_expert': nrm((L, D, N_EXPERTS), D ** -0.5),
        'b_route_expert': nrm((L, N_EXPERTS), 0.01),
        'w_gate': nrm((L, N_EXPERTS, D, EXPERT_FF), D ** -0.5),
        'w_up': nrm((L, N_EXPERTS, D, EXPERT_FF), D ** -0.5),
        'w_down': nrm((L, N_EXPERTS, EXPERT_FF, D), EXPERT_FF ** -0.5),
        'final_g': 1.0 + nrm((D,), 0.02),
    }


def reference(x, c, positions, ada_w, ada_b, norm_mix_g, w_in, tshift_mu, lambda_q1, lambda_k1,
              lambda_q2, lambda_k2, subln_g, w0, w2, a0, a2, g2, k_k, k_a, r_k, lnx_g, lnx_b,
              w_up_attn, w_up_rwkv, w_out, norm_ffn_g, w_route_group, b_route_group,
              w_route_expert, b_route_expert, w_gate, w_up, w_down, final_g):
    B, S, D = x.shape
    f32 = jnp.float32
    h = x
    for l in range(DEPTH):
        ada = jax.nn.silu(c) @ ada_w[l] + ada_b[l]
        sh1, sc1, gt1, sh2, sc2, gt2 = [t[:, None, :] for t in jnp.split(ada, 6, axis=-1)]
        u = rmsnorm(h, norm_mix_g[l]) * (1.0 + sc1) + sh1
        proj = u @ w_in[l]
        p_att, p_rwkv, p_gate = jnp.split(proj, [ATT_COLS, ATT_COLS + RWKV_COLS], axis=-1)
        q, k, v = jnp.split(p_att, [QK_COLS, 2 * QK_COLS], axis=-1)
        q = rope_partial(q.reshape(B, S, ATT_HEADS, 2, ATT_QK_DIM), positions)
        k = rope_partial(k.reshape(B, S, ATT_HEADS, 2, ATT_QK_DIM), positions)
        v = v.reshape(B, S, ATT_HEADS, ATT_V_DIM)
        lambda_init = 0.8 - 0.6 * math.exp(-0.3 * l)
        lam = (jnp.exp(jnp.sum(lambda_q1[l] * lambda_k1[l]).astype(f32))
               - jnp.exp(jnp.sum(lambda_q2[l] * lambda_k2[l]).astype(f32)) + lambda_init)
        y_att = diff_attention(q, k, v, lam, lambda_init, subln_g[l])
        y_rwkv = rwkv7_timemix(p_rwkv, tshift_mu[l], w0[l], w2[l], a0[l], a2[l], g2[l],
                               k_k[l], k_a[l], r_k[l], lnx_g[l], lnx_b[l])
        g_att, g_rwkv = jnp.split(jax.nn.sigmoid(p_gate), 2, axis=-1)
        merged = g_att * (y_att @ w_up_attn[l]) + g_rwkv * (y_rwkv @ w_up_rwkv[l])
        h = h + gt1 * (merged @ w_out[l])
        u2 = rmsnorm(h, norm_ffn_g[l]) * (1.0 + sc2) + sh2
        h = h + gt2 * hier_moe(u2, w_route_group[l], b_route_group[l], w_route_expert[l],
                               b_route_expert[l], w_gate[l], w_up[l], w_down[l])
    return rmsnorm(h, final_g)
```

```python
import functools
import math

import jax
import jax.numpy as jnp
from jax import lax
from jax.experimental import pallas as pl
from jax.experimental.pallas import tpu as pltpu

F32 = jnp.float32
BF16 = jnp.bfloat16
I32 = jnp.int32
HIGHEST = lax.Precision.HIGHEST

ATT_HEADS = 8
ATT_QK_DIM = 64
ATT_V_DIM = 128
ROPE_DIM = 16
ROPE_THETA = 500000.0
RWKV_HEAD = 64
DECAY_RANK = 64
ICLR_RANK = 64
GATE_RANK = 128
N_GROUPS = 8
EXPERTS_PER_GROUP = 8
N_EXPERTS = 64
TOP_K = 2
NORM_EPS = 1e-6
SUBLN_EPS = 1e-5
LNX_EPS = 64e-5

LANES = 128
SUBLANES = 8
VMEM_LIMIT = 56 * 1024 * 1024


def _cparams(sem, vmem=VMEM_LIMIT):
    return pltpu.CompilerParams(dimension_semantics=sem, vmem_limit_bytes=vmem)


def _sigmoid(x):
    return 1.0 / (1.0 + jnp.exp(-x))


def _ada_kernel(c_ref, w_ref, b_ref, o_ref):
    c = c_ref[...]
    s = c * _sigmoid(c)
    o_ref[...] = jnp.dot(s, w_ref[...], precision=HIGHEST, preferred_element_type=F32) + b_ref[...]


def _ada(c, w, b):
    B, D = c.shape
    N = w.shape[1]
    tn = 1024
    cp = jnp.zeros((8, D), F32).at[:B].set(c)
    out = pl.pallas_call(
        _ada_kernel,
        grid=(N // tn,),
        in_specs=[pl.BlockSpec((8, D), lambda j: (0, 0)),
                  pl.BlockSpec((D, tn), lambda j: (0, j)),
                  pl.BlockSpec((1, tn), lambda j: (0, j))],
        out_specs=pl.BlockSpec((8, tn), lambda j: (0, j)),
        out_shape=jax.ShapeDtypeStruct((8, N), F32),
        compiler_params=_cparams(("arbitrary",)),
        name="ada",
    )(cp, w, b.reshape(1, N))
    return out[:B]


def _modulated_norm(x, g, sc, sh):
    y = x * lax.rsqrt(jnp.mean(x * x, axis=-1, keepdims=True) + NORM_EPS) * g
    return y * (1.0 + sc) + sh


def _norm_proj_kernel(x_ref, g_ref, sc_ref, sh_ref, w_ref, o_ref, u_scr):
    @pl.when(pl.program_id(1) == 0)
    def _():
        u_scr[...] = _modulated_norm(x_ref[...], g_ref[...], sc_ref[0], sh_ref[0]).astype(BF16)

    o_ref[...] = jnp.dot(u_scr[...], w_ref[...], preferred_element_type=F32).astype(o_ref.dtype)


def _norm_proj(x2, g, sc, sh, w, out_dtype, S, tm, tn):
    N, D = x2.shape
    Nc = w.shape[1]
    bpt = S // tm
    return pl.pallas_call(
        _norm_proj_kernel,
        grid=(N // tm, Nc // tn),
        in_specs=[pl.BlockSpec((tm, D), lambda i, j: (i, 0)),
                  pl.BlockSpec((1, D), lambda i, j: (0, 0)),
                  pl.BlockSpec((1, 1, D), lambda i, j: (i // bpt, 0, 0)),
                  pl.BlockSpec((1, 1, D), lambda i, j: (i // bpt, 0, 0)),
                  pl.BlockSpec((D, tn), lambda i, j: (0, j))],
        out_specs=pl.BlockSpec((tm, tn), lambda i, j: (i, j)),
        out_shape=jax.ShapeDtypeStruct((N, Nc), out_dtype),
        scratch_shapes=[pltpu.VMEM((tm, D), BF16)],
        compiler_params=_cparams(("arbitrary", "arbitrary")),
        name="norm_proj",
    )(x2, g, sc, sh, w)


def _norm_proj_rope_kernel(x_ref, g_ref, sc_ref, sh_ref, pos_ref, inv_ref, w_ref, o_ref,
                           u_scr, c_scr, sa_scr, sb_scr, *, n_q_tiles, q_scale):
    j = pl.program_id(1)

    @pl.when(j == 0)
    def _():
        u_scr[...] = _modulated_norm(x_ref[...], g_ref[...], sc_ref[0], sh_ref[0]).astype(BF16)
        ang = pos_ref[...].astype(F32) * inv_ref[...]
        lane = lax.broadcasted_iota(I32, ang.shape, 1) % ATT_QK_DIM
        half = ROPE_DIM // 2
        cs, sn = jnp.cos(ang), jnp.sin(ang)
        c_scr[...] = jnp.where(lane < ROPE_DIM, cs, 1.0)
        sa_scr[...] = jnp.where(lane < half, -sn, 0.0)
        sb_scr[...] = jnp.where((lane >= half) & (lane < ROPE_DIM), sn, 0.0)

    acc = jnp.dot(u_scr[...], w_ref[...], preferred_element_type=F32)
    scale = jnp.where(j < n_q_tiles, q_scale, 1.0).astype(F32)
    half = ROPE_DIM // 2
    cols = []
    for gidx in range(acc.shape[1] // LANES):
        t = acc[:, gidx * LANES:(gidx + 1) * LANES]
        rot = (t * c_scr[...] + pltpu.roll(t, LANES - half, 1) * sa_scr[...]
               + pltpu.roll(t, half, 1) * sb_scr[...])
        cols.append((rot * scale).astype(o_ref.dtype))
    o_ref[...] = jnp.concatenate(cols, axis=1)


def _norm_proj_rope(x2, g, sc, sh, pos2, inv_lane, w, S, tm, tn, n_q_cols):
    N, D = x2.shape
    Nc = w.shape[1]
    bpt = S // tm
    kern = functools.partial(_norm_proj_rope_kernel, n_q_tiles=n_q_cols // tn, q_scale=ATT_QK_DIM ** -0.5)
    return pl.pallas_call(
        kern,
        grid=(N // tm, Nc // tn),
        in_specs=[pl.BlockSpec((tm, D), lambda i, j: (i, 0)),
                  pl.BlockSpec((1, D), lambda i, j: (0, 0)),
                  pl.BlockSpec((1, 1, D), lambda i, j: (i // bpt, 0, 0)),
                  pl.BlockSpec((1, 1, D), lambda i, j: (i // bpt, 0, 0)),
                  pl.BlockSpec((tm, 1), lambda i, j: (i, 0)),
                  pl.BlockSpec((1, LANES), lambda i, j: (0, 0)),
                  pl.BlockSpec((D, tn), lambda i, j: (0, j))],
        out_specs=pl.BlockSpec((tm, tn), lambda i, j: (i, j)),
        out_shape=jax.ShapeDtypeStruct((N, Nc), BF16),
        scratch_shapes=[pltpu.VMEM((tm, D), BF16), pltpu.VMEM((tm, LANES), F32),
                        pltpu.VMEM((tm, LANES), F32), pltpu.VMEM((tm, LANES), F32)],
        compiler_params=_cparams(("arbitrary", "arbitrary")),
        name="norm_proj_rope",
    )(x2, g, sc, sh, pos2, inv_lane, w)


def _attn_kernel(q_ref, k_ref, v_ref, lq1_ref, lk1_ref, lq2_ref, lk2_ref, g_ref, o_ref,
                 qm_scr, m_scr, l_scr, acc_scr, *, lambda_init):
    qi, ki = pl.program_id(2), pl.program_id(3)
    tq, tk = q_ref.shape[1], k_ref.shape[1]

    @pl.when(ki == 0)
    def _():
        q = q_ref[0]
        lane = lax.broadcasted_iota(I32, q.shape, 1)
        qm_scr[0] = jnp.where(lane < ATT_QK_DIM, q, jnp.zeros_like(q))
        qm_scr[1] = jnp.where(lane >= ATT_QK_DIM, q, jnp.zeros_like(q))
        m_scr[...] = jnp.full(m_scr.shape, -jnp.inf, F32)
        l_scr[...] = jnp.zeros(l_scr.shape, F32)
        acc_scr[...] = jnp.zeros(acc_scr.shape, F32)

    def block(masked):
        k = k_ref[0]
        v = v_ref[0]
        for c in range(2):
            s = lax.dot_general(qm_scr[c], k, (((1,), (1,)), ((), ())), preferred_element_type=F32)
            if masked:
                row = lax.broadcasted_iota(I32, s.shape, 0)
                col = lax.broadcasted_iota(I32, s.shape, 1)
                s = jnp.where(col <= row, s, -jnp.inf)
            m_prev = m_scr[c]
            m_new = jnp.maximum(m_prev, jnp.max(s, axis=-1, keepdims=True))
            alpha = jnp.exp(m_prev - m_new)
            p = jnp.exp(s - m_new)
            l_scr[c] = alpha * l_scr[c] + jnp.sum(p, axis=-1, keepdims=True)
            acc_scr[c] = alpha * acc_scr[c] + jnp.dot(p.astype(BF16), v, preferred_element_type=F32)
            m_scr[c] = m_new

    @pl.when(ki < qi)
    def _():
        block(False)

    @pl.when(ki == qi)
    def _():
        block(True)
        lam = (jnp.exp(jnp.sum(lq1_ref[...] * lk1_ref[...], axis=-1, keepdims=True))
               - jnp.exp(jnp.sum(lq2_ref[...] * lk2_ref[...], axis=-1, keepdims=True)) + lambda_init)
        o = acc_scr[0] / l_scr[0] - lam * (acc_scr[1] / l_scr[1])
        y = o * lax.rsqrt(jnp.mean(o * o, axis=-1, keepdims=True) + SUBLN_EPS) * g_ref[...]
        o_ref[0] = (y * (1.0 - lambda_init)).astype(o_ref.dtype)


def _diff_attention(qk, v, lq1, lk1, lq2, lk2, subln_g, lambda_init, tq):
    B, S, _ = v.shape
    H = ATT_HEADS
    nq = S // tq
    kern = functools.partial(_attn_kernel, lambda_init=lambda_init)
    vec = lambda a: a.reshape(1, ATT_QK_DIM)
    small = pl.BlockSpec((1, ATT_QK_DIM), lambda b, h, i, j: (0, 0))
    return pl.pallas_call(
        kern,
        grid=(B, H, nq, nq),
        in_specs=[pl.BlockSpec((1, tq, LANES), lambda b, h, i, j: (b, i, h)),
                  pl.BlockSpec((1, tq, LANES), lambda b, h, i, j: (b, jnp.minimum(i, j), H + h)),
                  pl.BlockSpec((1, tq, LANES), lambda b, h, i, j: (b, jnp.minimum(i, j), h)),
                  small, small, small, small,
                  pl.BlockSpec((1, ATT_V_DIM), lambda b, h, i, j: (0, 0))],
        out_specs=pl.BlockSpec((1, tq, LANES), lambda b, h, i, j: (b, i, h)),
        out_shape=jax.ShapeDtypeStruct((B, S, H * ATT_V_DIM), BF16),
        scratch_shapes=[pltpu.VMEM((2, tq, LANES), BF16), pltpu.VMEM((2, tq, 1), F32),
                        pltpu.VMEM((2, tq, 1), F32), pltpu.VMEM((2, tq, LANES), F32)],
        compiler_params=_cparams(("arbitrary", "arbitrary", "arbitrary", "arbitrary")),
        name="diff_attn",
    )(qk, qk, v, vec(lq1), vec(lk1), vec(lq2), vec(lk2), subln_g.reshape(1, ATT_V_DIM))


def _seg_ones():
    r = lax.broadcasted_iota(I32, (LANES, LANES), 0) // RWKV_HEAD
    c = lax.broadcasted_iota(I32, (LANES, LANES), 1) // RWKV_HEAD
    return jnp.where(r == c, 1.0, 0.0).astype(BF16)


def _seg_sum(x, q):
    hi = x.astype(BF16)
    lo = (x - hi.astype(F32)).astype(BF16)
    return (jnp.dot(hi, q, preferred_element_type=F32) + jnp.dot(lo, q, preferred_element_type=F32))


def _rwkv_prep_kernel(p_ref, prev_ref, mu_ref, wwa_ref, g2_ref, w0_ref, a0_ref, kk_ref, ka_ref, rk_ref,
                      r_o, w_o, k_o, v_o, na_o, b_o, g_o, bon_o, *, tiles_per_seq):
    i = pl.program_id(0)
    W = r_o.shape[1]
    p = p_ref[...]
    first = jnp.where(i % tiles_per_seq == 0, 0.0, 1.0).astype(F32)
    prev0 = prev_ref[7:8, :] * first
    row = lax.broadcasted_iota(I32, p.shape, 0)
    prev = jnp.where(row == 0, prev0, pltpu.roll(p, 1, 0))
    ps = p + (prev - p) * mu_ref[...]
    r = ps[:, 0:W]
    k = ps[:, W:2 * W]
    v = ps[:, 2 * W:3 * W]
    z = ps[:, 3 * W:3 * W + LANES]
    gl = ps[:, 3 * W + LANES:3 * W + 2 * LANES]
    lane = lax.broadcasted_iota(I32, z.shape, 1)
    zz = jnp.where(lane < DECAY_RANK, jnp.tanh(z), z).astype(BF16)
    lwa = jnp.dot(zz, wwa_ref[...], preferred_element_type=F32)
    y = -(w0_ref[...] + lwa[:, 0:W])
    softplus = jnp.maximum(y, 0.0) + jnp.log(1.0 + jnp.exp(-jnp.abs(y)))
    w_log = -softplus - 0.5
    decay = jnp.exp(-jnp.exp(w_log))
    a = _sigmoid(a0_ref[...] + lwa[:, W:2 * W])
    g = jnp.dot(_sigmoid(gl).astype(BF16), g2_ref[...], preferred_element_type=F32)
    kk = k * kk_ref[...]
    k2 = k * (1.0 + (a - 1.0) * ka_ref[...])
    q = _seg_ones()
    kk_cols, bon_cols = [], []
    for gi in range(W // LANES):
        sl = slice(gi * LANES, (gi + 1) * LANES)
        kkg = kk[:, sl]
        ss = _seg_sum(kkg * kkg, q)
        kk_cols.append(kkg * lax.rsqrt(jnp.maximum(ss, 1e-24)))
        bon_cols.append(_seg_sum(r[:, sl] * k2[:, sl] * rk_ref[:, sl], q) * v[:, sl])
    kkn = jnp.concatenate(kk_cols, axis=1)
    r_o[...] = r
    w_o[...] = decay
    k_o[...] = k2
    v_o[...] = v
    na_o[...] = -kkn
    b_o[...] = kkn * a
    g_o[...] = g
    bon_o[...] = jnp.concatenate(bon_cols, axis=1)


def _rwkv_prep(p2, mu, wwa, g2, w0, a0, k_k, k_a, r_k, S, tm):
    N, C = p2.shape
    W = w0.shape[-1]
    tps = S // tm
    kern = functools.partial(_rwkv_prep_kernel, tiles_per_seq=tps)
    row = lambda a: a.reshape(1, -1)
    cst = lambda shape: pl.BlockSpec(shape, lambda i: (0, 0))
    outs = pl.pallas_call(
        kern,
        grid=(N // tm,),
        in_specs=[pl.BlockSpec((tm, C), lambda i: (i, 0)),
                  pl.BlockSpec((8, C), lambda i: (jnp.maximum(i * (tm // 8) - 1, 0), 0)),
                  cst((1, C)), cst((LANES, 2 * W)), cst((GATE_RANK, W)),
                  cst((1, W)), cst((1, W)), cst((1, W)), cst((1, W)), cst((1, W))],
        out_specs=[pl.BlockSpec((tm, W), lambda i: (i, 0))] * 8,
        out_shape=[jax.ShapeDtypeStruct((N, W), F32)] * 8,
        compiler_params=_cparams(("arbitrary",)),
        name="rwkv_prep",
    )(p2, p2, row(mu), wwa, g2, row(w0), row(a0), row(k_k), row(k_a), row(r_k))
    return outs


RWKV_CHUNK = 64


def _rwkv_scan_kernel(r_ref, w_ref, k_ref, v_ref, a_ref, b_ref, o_ref, st_scr):
    B = r_ref.shape[0]
    npair = r_ref.shape[2] // LANES
    NP = B * npair

    @pl.when(pl.program_id(0) == 0)
    def _():
        st_scr[...] = jnp.zeros(st_scr.shape, F32)

    o_ref[...] = jnp.zeros(o_ref.shape, F32)
    q = _seg_ones()
    sub = lax.broadcasted_iota(I32, (RWKV_HEAD, LANES), 0)
    lane = lax.broadcasted_iota(I32, (RWKV_HEAD, LANES), 1)
    vmask = jnp.where(lane % RWKV_HEAD == sub, 1.0, 0.0).astype(F32)
    lane_t = lane % RWKV_HEAD

    def rows8(ref, base):
        return [ref[pidx // npair, pl.ds(base, SUBLANES), (pidx % npair) * LANES:(pidx % npair + 1) * LANES]
                for pidx in range(NP)]

    def group(t8, carry):
        base = pl.multiple_of(t8 * SUBLANES, SUBLANES)
        r8, w8, k8, v8, a8, b8 = (rows8(ref, base) for ref in (r_ref, w_ref, k_ref, v_ref, a_ref, b_ref))
        for j in range(SUBLANES):
            row = lambda tiles, pidx: tiles[pidx][j:j + 1, :]
            xs = [(st_scr[pidx] * row(a8, pidx)).astype(BF16) for pidx in range(NP)]
            vs = [(vmask * row(v8, pidx)).astype(BF16) for pidx in range(NP)]
            sa_all = jnp.dot(jnp.concatenate(xs, axis=0), q, preferred_element_type=F32)
            vb_all = jnp.dot(jnp.concatenate(vs, axis=0), q, preferred_element_type=F32)
            ys = []
            for pidx in range(NP):
                rs = slice(pidx * RWKV_HEAD, (pidx + 1) * RWKV_HEAD)
                s_new = (st_scr[pidx] * row(w8, pidx) + sa_all[rs] * row(b8, pidx) + vb_all[rs] * row(k8, pidx))
                st_scr[pidx] = s_new
                ys.append((s_new * row(r8, pidx)).astype(BF16))
            out_all = jnp.dot(jnp.concatenate(ys, axis=0), q, preferred_element_type=F32)
            sel = lane_t == base + j
            for pidx in range(NP):
                rs = slice(pidx * RWKV_HEAD, (pidx + 1) * RWKV_HEAD)
                o_ref[0, pidx] = jnp.where(sel, out_all[rs], o_ref[0, pidx])
        return carry

    lax.fori_loop(0, RWKV_CHUNK // SUBLANES, group, 0)


def _rwkv_scan(r, w, k, v, na, bb):
    B, S, W = r.shape
    NP = B * (W // LANES)
    nchunk = S // RWKV_CHUNK
    spec = pl.BlockSpec((B, RWKV_CHUNK, W), lambda c: (0, c, 0))
    return pl.pallas_call(
        _rwkv_scan_kernel,
        grid=(nchunk,),
        in_specs=[spec] * 6,
        out_specs=pl.BlockSpec((1, NP, RWKV_HEAD, LANES), lambda c: (c, 0, 0, 0)),
        out_shape=jax.ShapeDtypeStruct((nchunk, NP, RWKV_HEAD, LANES), F32),
        scratch_shapes=[pltpu.VMEM((NP, RWKV_HEAD, LANES), F32)],
        compiler_params=_cparams(("arbitrary",)),
        name="rwkv_scan",
    )(r, w, k, v, na, bb)


def _rwkv_post_kernel(o_ref, bon_ref, g_ref, lg_ref, lb_ref, y_ref):
    npair = o_ref.shape[2]
    for hp in range(npair):
        x = o_ref[0, 0, hp]
        mean = jnp.mean(x, axis=0, keepdims=True)
        xc = x - mean
        var = jnp.mean(xc * xc, axis=0, keepdims=True)
        xn = xc * lax.rsqrt(var + LNX_EPS)
        xt = xn.T
        rows = jnp.concatenate([xt[:RWKV_HEAD], xt[RWKV_HEAD:]], axis=1)
        sl = slice(hp * LANES, (hp + 1) * LANES)
        y = (rows * lg_ref[:, sl] + lb_ref[:, sl] + bon_ref[0, :, sl]) * g_ref[0, :, sl]
        y_ref[0, :, sl] = y.astype(y_ref.dtype)


def _rwkv_post(oT, bonus, g, lnx_g, lnx_b):
    B, S, W = bonus.shape
    npair = W // LANES
    nchunk = S // RWKV_CHUNK
    o5 = oT.reshape(nchunk, B, npair, RWKV_HEAD, LANES)
    rows = pl.BlockSpec((1, RWKV_CHUNK, W), lambda b, c: (b, c, 0))
    cst = pl.BlockSpec((1, W), lambda b, c: (0, 0))
    return pl.pallas_call(
        _rwkv_post_kernel,
        grid=(B, nchunk),
        in_specs=[pl.BlockSpec((1, 1, npair, RWKV_HEAD, LANES), lambda b, c: (c, b, 0, 0, 0)),
                  rows, rows, cst, cst],
        out_specs=rows,
        out_shape=jax.ShapeDtypeStruct((B, S, W), BF16),
        compiler_params=_cparams(("arbitrary", "arbitrary")),
        name="rwkv_post",
    )(o5, bonus, g, lnx_g.reshape(1, W), lnx_b.reshape(1, W))


def _merge_kernel(ya_ref, yb_ref, ga_ref, gb_ref, wa_ref, wb_ref, o_ref):
    ma = jnp.dot(ya_ref[...], wa_ref[...], preferred_element_type=F32)
    mb = jnp.dot(yb_ref[...], wb_ref[...], preferred_element_type=F32)
    o_ref[...] = (_sigmoid(ga_ref[...]) * ma + _sigmoid(gb_ref[...]) * mb).astype(o_ref.dtype)


def _merge(ya, yb, gate, wa, wb, tm):
    N, Wa = ya.shape
    D = wa.shape[1]
    return pl.pallas_call(
        _merge_kernel,
        grid=(N // tm,),
        in_specs=[pl.BlockSpec((tm, Wa), lambda i: (i, 0)),
                  pl.BlockSpec((tm, yb.shape[1]), lambda i: (i, 0)),
                  pl.BlockSpec((tm, D), lambda i: (i, 0)),
                  pl.BlockSpec((tm, D), lambda i: (i, 1)),
                  pl.BlockSpec(wa.shape, lambda i: (0, 0)),
                  pl.BlockSpec(wb.shape, lambda i: (0, 0))],
        out_specs=pl.BlockSpec((tm, D), lambda i: (i, 0)),
        out_shape=jax.ShapeDtypeStruct((N, D), BF16),
        compiler_params=_cparams(("arbitrary",)),
        name="merge",
    )(ya, yb, gate, gate, wa, wb)


def _outproj_kernel(m_ref, x_ref, gt_ref, g_ref, sc_ref, sh_ref, w_ref, wr_ref, h_ref, u_ref, lg_ref):
    h = x_ref[...] + gt_ref[0] * jnp.dot(m_ref[...], w_ref[...], preferred_element_type=F32)
    h_ref[...] = h
    u = _modulated_norm(h, g_ref[...], sc_ref[0], sh_ref[0])
    u_ref[...] = u
    lg_ref[...] = jnp.dot(u, wr_ref[...], precision=HIGHEST, preferred_element_type=F32)


def _outproj(merged, x2, gt1, g, sc2, sh2, w_out, w_route, S, tm):
    N, D = x2.shape
    bpt = S // tm
    mod = pl.BlockSpec((1, 1, D), lambda i: (i // bpt, 0, 0))
    rows = pl.BlockSpec((tm, D), lambda i: (i, 0))
    return pl.pallas_call(
        _outproj_kernel,
        grid=(N // tm,),
        in_specs=[rows, rows, mod, pl.BlockSpec((1, D), lambda i: (0, 0)), mod, mod,
                  pl.BlockSpec((D, D), lambda i: (0, 0)),
                  pl.BlockSpec((D, LANES), lambda i: (0, 0))],
        out_specs=[rows, rows, pl.BlockSpec((tm, LANES), lambda i: (i, 0))],
        out_shape=[jax.ShapeDtypeStruct((N, D), F32), jax.ShapeDtypeStruct((N, D), F32),
                   jax.ShapeDtypeStruct((N, LANES), F32)],
        compiler_params=_cparams(("arbitrary",)),
        name="outproj",
    )(merged, x2, gt1, g, sc2, sh2, w_out, w_route)


def _route_kernel(lg_ref, b_ref, id_ref, w_ref):
    z = lg_ref[...] + b_ref[...]
    lane = lax.broadcasted_iota(I32, z.shape, 1)
    big = jnp.int32(1 << 20)
    gmask = lane < N_GROUPS
    mg = jnp.max(jnp.where(gmask, z, -jnp.inf), axis=-1, keepdims=True)
    eg = jnp.where(gmask, jnp.exp(z - mg), 0.0)
    gprob = eg / jnp.sum(eg, axis=-1, keepdims=True)
    gp = jnp.max(gprob, axis=-1, keepdims=True)
    gidx = jnp.min(jnp.where(gmask & (gprob == gp), lane, big), axis=-1, keepdims=True)
    emask = (lane >= N_GROUPS) & (lane < N_GROUPS + N_EXPERTS) & ((lane - N_GROUPS) // EXPERTS_PER_GROUP == gidx)
    me = jnp.max(jnp.where(emask, z, -jnp.inf), axis=-1, keepdims=True)
    ee = jnp.where(emask, jnp.exp(z - me), 0.0)
    eprob = ee / jnp.sum(ee, axis=-1, keepdims=True)
    p0 = jnp.max(jnp.where(emask, eprob, -1.0), axis=-1, keepdims=True)
    i0 = jnp.min(jnp.where(emask & (eprob == p0), lane, big), axis=-1, keepdims=True)
    m1 = emask & (lane != i0)
    p1 = jnp.max(jnp.where(m1, eprob, -1.0), axis=-1, keepdims=True)
    i1 = jnp.min(jnp.where(m1 & (eprob == p1), lane, big), axis=-1, keepdims=True)
    den = p0 + p1
    id_ref[...] = jnp.where(lane == 0, i0 - N_GROUPS, jnp.where(lane == 1, i1 - N_GROUPS, 0))
    w_ref[...] = jnp.where(lane == 0, gp * p0 / den, jnp.where(lane == 1, gp * p1 / den, 0.0))


def _route(logits, bias, tm):
    N = logits.shape[0]
    rows = pl.BlockSpec((tm, LANES), lambda i: (i, 0))
    return pl.pallas_call(
        _route_kernel,
        grid=(N // tm,),
        in_specs=[rows, pl.BlockSpec((1, LANES), lambda i: (0, 0))],
        out_specs=[rows, rows],
        out_shape=[jax.ShapeDtypeStruct((N, LANES), I32), jax.ShapeDtypeStruct((N, LANES), F32)],
        compiler_params=_cparams(("arbitrary",)),
        name="route",
    )(logits, bias)


def _rank_kernel(id_ref, rank_ref, cnt_ref, carry_scr):
    kk, i = pl.program_id(0), pl.program_id(1)

    @pl.when((kk == 0) & (i == 0))
    def _():
        carry_scr[...] = jnp.zeros(carry_scr.shape, F32)

    ids = id_ref[...]
    ta = ids.shape[0]
    lane = lax.broadcasted_iota(I32, ids.shape, 1)
    e = jnp.sum(jnp.where(lane == kk, ids, 0), axis=-1, keepdims=True)
    onehot = lane == e
    oh = jnp.where(onehot, 1.0, 0.0)
    r = lax.broadcasted_iota(I32, (ta, ta), 0)
    c = lax.broadcasted_iota(I32, (ta, ta), 1)
    tri = jnp.where(c < r, 1.0, 0.0).astype(BF16)
    prefix = jnp.dot(tri, oh.astype(BF16), preferred_element_type=F32) + carry_scr[0:1, :]
    rank = jnp.sum(jnp.where(onehot, prefix, 0.0), axis=-1, keepdims=True)
    rank_ref[0] = jnp.broadcast_to(rank, ids.shape).astype(I32)
    carry_scr[0:1, :] = carry_scr[0:1, :] + jnp.sum(oh, axis=0, keepdims=True)
    cnt_ref[...] = carry_scr[...].astype(I32)


def _rank(ids, ta):
    N = ids.shape[0]
    return pl.pallas_call(
        _rank_kernel,
        grid=(TOP_K, N // ta),
        in_specs=[pl.BlockSpec((ta, LANES), lambda k, i: (i, 0))],
        out_specs=[pl.BlockSpec((1, ta, LANES), lambda k, i: (k, i, 0)),
                   pl.BlockSpec((8, LANES), lambda k, i: (0, 0))],
        out_shape=[jax.ShapeDtypeStruct((TOP_K, N, LANES), I32), jax.ShapeDtypeStruct((8, LANES), I32)],
        scratch_shapes=[pltpu.VMEM((8, LANES), F32)],
        compiler_params=_cparams(("arbitrary", "arbitrary")),
        name="rank",
    )(ids)


def _gather_rows_kernel(tok_ref, src_ref, o_ref, sem):
    tm = o_ref.shape[0]

    def copy(r):
        return pltpu.make_async_copy(src_ref.at[pl.ds(tok_ref[0, 0, r], 1)], o_ref.at[pl.ds(r, 1)], sem)

    def start(r, c):
        copy(r).start()
        return c

    def wait(r, c):
        copy(r).wait()
        return c

    lax.fori_loop(0, tm, start, 0)
    lax.fori_loop(0, tm, wait, 0)


def _gather_rows(src, tok, tm):
    D = src.shape[1]
    nb = tok.shape[0] // tm
    return pl.pallas_call(
        _gather_rows_kernel,
        grid=(nb,),
        in_specs=[pl.BlockSpec((1, 1, tm), lambda b: (b, 0, 0), memory_space=pltpu.SMEM),
                  pl.BlockSpec(memory_space=pl.ANY)],
        out_specs=pl.BlockSpec((tm, D), lambda b: (b, 0)),
        out_shape=jax.ShapeDtypeStruct((nb * tm, D), src.dtype),
        scratch_shapes=[pltpu.SemaphoreType.DMA],
        compiler_params=_cparams(("arbitrary",)),
        name="moe_gather",
    )(tok.reshape(nb, 1, tm), src)


def _ffn_kernel(be_ref, nu_ref, x_ref, rw_ref, wg_ref, wu_ref, wd_ref, o_ref, xb_scr):
    b, c = pl.program_id(0), pl.program_id(1)
    nc = pl.num_programs(1)

    @pl.when(b < nu_ref[0])
    def _():
        @pl.when(c == 0)
        def _():
            xb_scr[...] = x_ref[...].astype(BF16)
            o_ref[...] = jnp.zeros(o_ref.shape, F32)

        xb = xb_scr[...]
        gate = jnp.dot(xb, wg_ref[0].astype(BF16), preferred_element_type=F32)
        up = jnp.dot(xb, wu_ref[0].astype(BF16), preferred_element_type=F32)
        hmid = (gate * _sigmoid(gate) * up).astype(BF16)
        o_ref[...] += jnp.dot(hmid, wd_ref[0].astype(BF16), preferred_element_type=F32)

        @pl.when(c == nc - 1)
        def _():
            o_ref[...] = o_ref[...] * rw_ref[...]


def _expert_ffn(xs, row_w, blk_expert, n_used, w_gate, w_up, w_down, tm, tf):
    R, D = xs.shape
    nb = R // tm
    FF = w_gate.shape[2]

    def blk(b, c, be, nu):
        return jnp.minimum(b, nu[0] - 1)

    grid_spec = pltpu.PrefetchScalarGridSpec(
        num_scalar_prefetch=2,
        grid=(nb, FF // tf),
        in_specs=[pl.BlockSpec((tm, D), lambda b, c, be, nu: (blk(b, c, be, nu), 0)),
                  pl.BlockSpec((tm, 1), lambda b, c, be, nu: (blk(b, c, be, nu), 0)),
                  pl.BlockSpec((1, D, tf), lambda b, c, be, nu: (be[blk(b, c, be, nu)], 0, jnp.where(b < nu[0], c, FF // tf - 1))),
                  pl.BlockSpec((1, D, tf), lambda b, c, be, nu: (be[blk(b, c, be, nu)], 0, jnp.where(b < nu[0], c, FF // tf - 1))),
                  pl.BlockSpec((1, tf, D), lambda b, c, be, nu: (be[blk(b, c, be, nu)], jnp.where(b < nu[0], c, FF // tf - 1), 0))],
        out_specs=pl.BlockSpec((tm, D), lambda b, c, be, nu: (blk(b, c, be, nu), 0)),
        scratch_shapes=[pltpu.VMEM((tm, D), BF16)],
    )
    return pl.pallas_call(
        _ffn_kernel,
        grid_spec=grid_spec,
        out_shape=jax.ShapeDtypeStruct((R, D), F32),
        compiler_params=_cparams(("arbitrary", "arbitrary")),
        name="moe_ffn",
    )(blk_expert, n_used, xs, row_w, w_gate, w_up, w_down)


def _combine_kernel(dest_ref, h_ref, gt_ref, fg_ref, ys_ref, o_ref, buf, sem):
    tb = h_ref.shape[0]

    def copy(k, r):
        return pltpu.make_async_copy(ys_ref.at[pl.ds(dest_ref[0, k, r], 1)], buf.at[k, pl.ds(r, 1)], sem)

    def start(r, c):
        copy(0, r).start()
        copy(1, r).start()
        return c

    def wait(r, c):
        copy(0, r).wait()
        copy(1, r).wait()
        return c

    lax.fori_loop(0, tb, start, 0)
    lax.fori_loop(0, tb, wait, 0)
    h = h_ref[...] + gt_ref[0] * (buf[0] + buf[1])
    o_ref[...] = h * lax.rsqrt(jnp.mean(h * h, axis=-1, keepdims=True) + NORM_EPS) * fg_ref[...]


def _combine(dest, h1, gt2, final_g, ys, S, tb):
    N, D = h1.shape
    nt = N // tb
    bpt = S // tb
    dest3 = dest.reshape(TOP_K, nt, tb).transpose(1, 0, 2)
    return pl.pallas_call(
        _combine_kernel,
        grid=(nt,),
        in_specs=[pl.BlockSpec((1, TOP_K, tb), lambda i: (i, 0, 0), memory_space=pltpu.SMEM),
                  pl.BlockSpec((tb, D), lambda i: (i, 0)),
                  pl.BlockSpec((1, 1, D), lambda i: (i // bpt, 0, 0)),
                  pl.BlockSpec((1, D), lambda i: (0, 0)),
                  pl.BlockSpec(memory_space=pl.ANY)],
        out_specs=pl.BlockSpec((tb, D), lambda i: (i, 0)),
        out_shape=jax.ShapeDtypeStruct((N, D), F32),
        scratch_shapes=[pltpu.VMEM((TOP_K, tb, D), F32), pltpu.SemaphoreType.DMA],
        compiler_params=_cparams(("arbitrary",)),
        name="moe_combine",
    )(dest3, h1, gt2, final_g.reshape(1, D), ys)


def _hier_moe_combine(u2, logits, h1, gt2, final_g, b_rg, b_re, w_gate, w_up, w_down, S):
    N, D = u2.shape
    tm = min(512, N)
    bias = jnp.zeros((1, LANES), F32).at[0, :N_GROUPS].set(b_rg).at[0, N_GROUPS:N_GROUPS + N_EXPERTS].set(b_re)
    ids, wts = _route(logits, bias, min(512, N))
    rank, counts = _rank(ids, min(512, N))
    counts = counts[0, :N_EXPERTS]
    padded = (counts + tm - 1) // tm * tm
    pad_end = jnp.cumsum(padded)
    pad_start = pad_end - padded
    nb = (TOP_K * N) // tm + N_EXPERTS
    n_used = (pad_end[-1] // tm).astype(I32).reshape(1)
    blk_expert = jnp.minimum(jnp.searchsorted(pad_end, jnp.arange(nb, dtype=I32) * tm, side='right'),
                             N_EXPERTS - 1).astype(I32)
    eid = ids[:, :TOP_K].T
    dest = pad_start[eid] + rank[:, :, 0]
    tok = jnp.broadcast_to(jnp.arange(N, dtype=I32)[None, :], (TOP_K, N))
    buf_tok = jnp.zeros((nb * tm,), I32).at[dest.reshape(-1)].set(tok.reshape(-1))
    buf_w = jnp.zeros((nb * tm,), F32).at[dest.reshape(-1)].set(wts[:, :TOP_K].T.reshape(-1))
    xs = _gather_rows(u2, buf_tok, tm)
    ys = _expert_ffn(xs, buf_w.reshape(-1, 1), blk_expert, n_used, w_gate, w_up, w_down, tm, 256)
    return _combine(dest.astype(I32), h1, gt2, final_g, ys, S, min(256, N))


def kernel(x, c, positions, ada_w, ada_b, norm_mix_g, w_in, tshift_mu, lambda_q1, lambda_k1, lambda_q2, lambda_k2, subln_g, w0, w2, a0, a2, g2, k_k, k_a, r_k, lnx_g, lnx_b, w_up_attn, w_up_rwkv, w_out, norm_ffn_g, w_route_group, b_route_group, w_route_expert, b_route_expert, w_gate, w_up, w_down, final_g):
    B, S, D = x.shape
    L = ada_w.shape[0]
    N = B * S
    H = ATT_HEADS
    qk_cols = 2 * H * 2 * ATT_QK_DIM
    att_cols = qk_cols + H * ATT_V_DIM
    W = w0.shape[-1]
    rwkv_cols = 3 * W + DECAY_RANK + ICLR_RANK + GATE_RANK

    half = ROPE_DIM // 2
    inv = ROPE_THETA ** (-jnp.arange(half, dtype=F32) / half)
    inv_lane = jnp.tile(inv, LANES // half).reshape(1, LANES)
    pos2 = positions.reshape(N, 1)

    h2d = x.reshape(N, D)
    out = None
    for l in range(L):
        ada = _ada(c, ada_w[l], ada_b[l])
        sh1, sc1, gt1, sh2, sc2, gt2 = [t.reshape(B, 1, D) for t in jnp.split(ada, 6, axis=-1)]
        w_bf = w_in[l].astype(BF16)
        g1 = norm_mix_g[l].reshape(1, D)
        tm = min(1024, S)
        qk = _norm_proj_rope(h2d, g1, sc1, sh1, pos2, inv_lane, w_bf[:, :qk_cols], S, tm, 512, qk_cols // 2)
        vv = _norm_proj(h2d, g1, sc1, sh1, w_bf[:, qk_cols:att_cols], BF16, S, tm, 512)
        p_rwkv = _norm_proj(h2d, g1, sc1, sh1, w_bf[:, att_cols:att_cols + rwkv_cols], F32, S, tm, 256)
        p_gate = _norm_proj(h2d, g1, sc1, sh1, w_bf[:, att_cols + rwkv_cols:], F32, S, tm, 512)

        lambda_init = 0.8 - 0.6 * math.exp(-0.3 * l)
        y_att = _diff_attention(qk.reshape(B, S, qk_cols), vv.reshape(B, S, H * ATT_V_DIM),
                                lambda_q1[l], lambda_k1[l], lambda_q2[l], lambda_k2[l], subln_g[l],
                                lambda_init, min(512, S))

        wwa = jnp.zeros((LANES, 2 * W), F32).at[:DECAY_RANK, :W].set(w2[l]).at[DECAY_RANK:, W:].set(a2[l]).astype(BF16)
        r_, w_, k_, v_, na_, b_, g_, bon_ = _rwkv_prep(p_rwkv, tshift_mu[l], wwa, g2[l].astype(BF16), w0[l], a0[l],
                                                       k_k[l], k_a[l], r_k[l].reshape(-1), S, min(256, S))
        sh3 = lambda t: t.reshape(B, S, W)
        oT = _rwkv_scan(sh3(r_), sh3(w_), sh3(k_), sh3(v_), sh3(na_), sh3(b_))
        y_rwkv = _rwkv_post(oT, sh3(bon_), sh3(g_), lnx_g[l], lnx_b[l])

        merged = _merge(y_att.reshape(N, -1), y_rwkv.reshape(N, W), p_gate,
                        w_up_attn[l].astype(BF16), w_up_rwkv[l].astype(BF16), min(256, N))
        w_route = jnp.zeros((D, LANES), F32).at[:, :N_GROUPS].set(w_route_group[l]).at[:, N_GROUPS:N_GROUPS + N_EXPERTS].set(w_route_expert[l])
        h1, u2, logits = _outproj(merged, h2d, gt1, norm_ffn_g[l].reshape(1, D), sc2, sh2,
                                  w_out[l].astype(BF16), w_route, S, min(256, S))
        if l == L - 1:
            out = _hier_moe_combine(u2, logits, h1, gt2, final_g, b_route_group[l], b_route_expert[l],
                                    w_gate[l], w_up[l], w_down[l], S)
        else:
            ones = jnp.ones_like(final_g)
            raise NotImplementedError("DEPTH > 1 is not part of this problem")
    return out.reshape(B, S, D)
```
